```python
import jax, jax.numpy as jnp
from jax import lax
import numpy as np


D_MODEL = 1024
BATCH = 8
SEQ = 4096
DEPTH = 2

CHUNK = 64
EPS = 1e-6
PLE_DIM = 256
D_FF = 2816
FFN_RES = 0.5
N_EVEN = (DEPTH + 1) // 2
N_ODD = DEPTH // 2

GLA_HEADS = 4
GLA_V = D_MODEL // 2
GLA_DV = GLA_V // GLA_HEADS
GLA_QK = GLA_V // 2
GLA_DK = GLA_QK // GLA_HEADS
GLA_RANK = 16
GLA_GATE_TEMP = 16.0

LRU_WIDTH = D_MODEL // 2
LRU_BLOCKS = 8
LRU_BW = LRU_WIDTH // LRU_BLOCKS
CONV_W = 4
LRU_C = 8.0

MIX0_SPLITS = [int(c) for c in np.cumsum([GLA_QK, GLA_QK, GLA_V, GLA_V, GLA_RANK, LRU_WIDTH])]
MIX0_COLS = 2 * GLA_QK + 2 * GLA_V + GLA_RANK + 2 * LRU_WIDTH
MIX0_OUT = GLA_V + LRU_WIDTH

HG_DK = 128
HG_HEADS = D_MODEL // HG_DK
HG_DV = 128
HG_K = HG_HEADS * HG_DK
HG_V = HG_HEADS * HG_DV
MIX1_SPLITS = [HG_K, 2 * HG_K, 2 * HG_K + HG_V]
MIX1_COLS = 2 * HG_K + 2 * HG_V

kernel_name = 'hybrid_gla_rglru_hgrn2_macaron_ple'


def rmsnorm(x, g):
    xf = x.astype(jnp.float32)
    y = xf * lax.rsqrt(jnp.mean(xf * xf, axis=-1, keepdims=True) + EPS)
    return (y * g.astype(jnp.float32)).astype(x.dtype)


def swiglu(h, w1, w3, w2):
    return (jax.nn.silu(h @ w1) * (h @ w3)) @ w2


def chunk_end_gated_linear_attention(q, k, v, log_f):
    b, s, h, dk = q.shape
    dv = v.shape[-1]
    n = s // CHUNK

    def chunks(t):
        return jnp.moveaxis(t.astype(jnp.float32).reshape(b, n, CHUNK, h, t.shape[-1]), 1, 0)

    qc, kc, vc, lf = chunks(q), chunks(k), chunks(v), chunks(log_f)
    cum = jnp.cumsum(lf, axis=2)
    total = cum[:, :, -1]
    k_dec = kc * jnp.exp(total[:, :, None] - cum)
    g_chunk = jnp.exp(total)

    def step(state, xs):
        q_c, k_c, v_c, g_c = xs
        state = g_c[..., None] * state + jnp.einsum('bchk,bchv->bhkv', k_c, v_c)
        return state, jnp.einsum('bchk,bhkv->bchv', q_c, state)

    s0 = jnp.zeros((b, h, dk, dv), jnp.float32)
    _, o = lax.scan(step, s0, (qc, k_dec, vc, g_chunk))
    return jnp.moveaxis(o, 0, 1).reshape(b, s, h, dv).astype(v.dtype)


def causal_depthwise_conv(x, w, bias):
    s = x.shape[1]
    xp = jnp.pad(x, ((0, 0), (CONV_W - 1, 0), (0, 0)))
    out = xp[:, 0:s] * w[0]
    for j in range(1, CONV_W):
        out = out + xp[:, j:j + s] * w[j]
    return out + bias


def block_diag(x, w, bias):
    b, s, wd = x.shape
    xb = x.reshape(b, s, LRU_BLOCKS, LRU_BW)
    return jnp.einsum('bsgi,gij->bsgj', xb, w).reshape(b, s, wd) + bias


def rg_lru(x, w_a, b_a, w_x, b_x, lam):
    r = jax.nn.sigmoid(block_diag(x, w_a, b_a)).astype(jnp.float32)
    i = jax.nn.sigmoid(block_diag(x, w_x, b_x))
    log_a = LRU_C * r * jax.nn.log_sigmoid(lam.astype(jnp.float32))
    a = jnp.exp(log_a)
    u = jnp.sqrt(-jnp.expm1(2.0 * log_a)) * (i * x).astype(jnp.float32)

    def combine(left, right):
        a1, b1 = left
        a2, b2 = right
        return a1 * a2, a2 * b1 + b2

    _, hs = lax.associative_scan(combine, (a, u), axis=1)
    return hs.astype(x.dtype)


def mixer_gla_rglru(h, w_in, gla_gate_up, gla_gate_bias, gla_head_norm, conv_w, conv_b,
                    lru_wa, lru_ba, lru_wx, lru_bx, lru_lambda, w_out):
    b, s, _ = h.shape
    z = h @ w_in
    q, k, v, g, lr, xr, xg = jnp.split(z, MIX0_SPLITS, axis=-1)
    log_f = jax.nn.log_sigmoid((lr @ gla_gate_up + gla_gate_bias).astype(jnp.float32)) / GLA_GATE_TEMP
    o = chunk_end_gated_linear_attention(
        q.reshape(b, s, GLA_HEADS, GLA_DK) * (GLA_DK ** -0.5),
        k.reshape(b, s, GLA_HEADS, GLA_DK),
        v.reshape(b, s, GLA_HEADS, GLA_DV),
        log_f.reshape(b, s, GLA_HEADS, GLA_DK))
    o = rmsnorm(o, gla_head_norm).reshape(b, s, GLA_V) * jax.nn.silu(g)
    y = rg_lru(causal_depthwise_conv(xr, conv_w, conv_b), lru_wa, lru_ba, lru_wx, lru_bx, lru_lambda)
    y = y * jax.nn.gelu(xg)
    return jnp.concatenate([o, y], axis=-1) @ w_out


def mixer_hgrn2(h, w_in, lb, hg_head_norm, w_out):
    b, s, _ = h.shape
    z = h @ w_in
    q, fz, i, g = jnp.split(z, MIX1_SPLITS, axis=-1)
    lbf = lb.astype(jnp.float32)
    zf = fz.astype(jnp.float32)
    log_f = jnp.logaddexp(jnp.log(lbf), jnp.log1p(-lbf) + jax.nn.log_sigmoid(zf))
    k = (1.0 - lbf) * jax.nn.sigmoid(-zf)
    o = chunk_end_gated_linear_attention(
        jax.nn.silu(q).reshape(b, s, HG_HEADS, HG_DK),
        k.astype(h.dtype).reshape(b, s, HG_HEADS, HG_DK),
        i.reshape(b, s, HG_HEADS, HG_DV),
        log_f.reshape(b, s, HG_HEADS, HG_DK))
    o = rmsnorm(o, hg_head_norm).reshape(b, s, HG_V) * jax.nn.silu(g)
    return o @ w_out


def lower_bounds(lb_logits):
    sm = jax.nn.softmax(lb_logits.astype(jnp.float32), axis=0)
    return jnp.cumsum(sm, axis=0) - sm[0]


def half_ffn(x, g, w1, w3, w2):
    return x + FFN_RES * swiglu(rmsnorm(x, g), w1, w3, w2)


def setup_inputs(seed: int = 0) -> dict:
    key = jax.random.key(seed)
    ks = iter(list(jax.random.split(key, 40)))

    def nrm(shape, scale):
        return jax.random.normal(next(ks), shape, jnp.float32) * scale

    def gain(shape):
        return 1.0 + nrm(shape, 0.05)

    a0 = jax.random.uniform(next(ks), (N_EVEN, LRU_WIDTH), jnp.float32, 0.9, 0.999)
    return {
        'x': nrm((BATCH, SEQ, D_MODEL), 1.0),
        'p': nrm((DEPTH, BATCH, SEQ, PLE_DIM), 1.0),
        'ffn_norm': gain((DEPTH, 2, D_MODEL)),
        'ffn_w1': nrm((DEPTH, 2, D_MODEL, D_FF), D_MODEL ** -0.5),
        'ffn_w3': nrm((DEPTH, 2, D_MODEL, D_FF), D_MODEL ** -0.5),
        'ffn_w2': nrm((DEPTH, 2, D_FF, D_MODEL), D_FF ** -0.5),
        'mix_norm': gain((DEPTH, D_MODEL)),
        'ple_norm': gain((DEPTH, D_MODEL)),
        'ple_proj': nrm((DEPTH, PLE_DIM, D_MODEL), PLE_DIM ** -0.5),
        'ple_gate': nrm((DEPTH, D_MODEL, D_MODEL), D_MODEL ** -0.5),
        'final_norm': gain((D_MODEL,)),
        'm0_w_in': nrm((N_EVEN, D_MODEL, MIX0_COLS), D_MODEL ** -0.5),
        'gla_gate_up': nrm((N_EVEN, GLA_RANK, GLA_QK), GLA_RANK ** -0.5),
        'gla_gate_bias': nrm((N_EVEN, GLA_QK), 0.1),
        'gla_head_norm': gain((N_EVEN, GLA_DV)),
        'lru_conv_w': nrm((N_EVEN, CONV_W, LRU_WIDTH), CONV_W ** -0.5),
        'lru_conv_b': nrm((N_EVEN, LRU_WIDTH), 0.02),
        'lru_wa': nrm((N_EVEN, LRU_BLOCKS, LRU_BW, LRU_BW), LRU_BW ** -0.5),
        'lru_ba': nrm((N_EVEN, LRU_WIDTH), 0.02),
        'lru_wx': nrm((N_EVEN, LRU_BLOCKS, LRU_BW, LRU_BW), LRU_BW ** -0.5),
        'lru_bx': nrm((N_EVEN, LRU_WIDTH), 0.02),
        'lru_lambda': jnp.log(a0) - jnp.log1p(-a0),
        'm0_w_out': nrm((N_EVEN, MIX0_OUT, D_MODEL), MIX0_OUT ** -0.5),
        'm1_w_in': nrm((N_ODD, D_MODEL, MIX1_COLS), D_MODEL ** -0.5),
        'hgrn_lb_logits': nrm((DEPTH, HG_K), 0.1),
        'hgrn_head_norm': gain((N_ODD, HG_DV)),
        'm1_w_out': nrm((N_ODD, HG_V, D_MODEL), HG_V ** -0.5),
    }


def reference(x, p, ffn_norm, ffn_w1, ffn_w3, ffn_w2, mix_norm, ple_norm, ple_proj, ple_gate,
              final_norm, m0_w_in, gla_gate_up, gla_gate_bias, gla_head_norm, lru_conv_w,
              lru_conv_b, lru_wa, lru_ba, lru_wx, lru_bx, lru_lambda, m0_w_out, m1_w_in,
              hgrn_lb_logits, hgrn_head_norm, m1_w_out):
    lbs = lower_bounds(hgrn_lb_logits)
    for layer in range(DEPTH):
        x = half_ffn(x, ffn_norm[layer, 0], ffn_w1[layer, 0], ffn_w3[layer, 0], ffn_w2[layer, 0])
        h = rmsnorm(x, mix_norm[layer])
        if layer % 2 == 0:
            e = layer // 2
            x = x + mixer_gla_rglru(h, m0_w_in[e], gla_gate_up[e], gla_gate_bias[e], gla_head_norm[e],
                                    lru_conv_w[e], lru_conv_b[e], lru_wa[e], lru_ba[e], lru_wx[e],
                                    lru_bx[e], lru_lambda[e], m0_w_out[e])
        else:
            o = layer // 2
            x = x + mixer_hgrn2(h, m1_w_in[o], lbs[layer], hgrn_head_norm[o], m1_w_out[o])
        x = half_ffn(x, ffn_norm[layer, 1], ffn_w1[layer, 1], ffn_w3[layer, 1], ffn_w2[layer, 1])
        gate = jax.nn.sigmoid(rmsnorm(x, ple_norm[layer]) @ ple_gate[layer])
        x = x + gate * (p[layer] @ ple_proj[layer])
    return rmsnorm(x, final_norm)
```

```python
import functools

import jax
import jax.numpy as jnp
from jax import lax
from jax.experimental import pallas as pl
from jax.experimental.pallas import tpu as pltpu

F32 = jnp.float32
BF16 = jnp.bfloat16

EPS = 1e-6
CHUNK = 64
FFN_RES = 0.5
GLA_HEADS = 4
GLA_GATE_TEMP = 16.0
LRU_C = 8.0
CONV_W = 4
HG_DK = 128
HG_DV = 128

LANE = 128
SUBLANE = 8
VMEM_LIMIT = 56 * 1024 * 1024

TOKEN_TILE = 512
SEQ_TILE = 512


def _dot(a, b):
    return jnp.dot(a, b, preferred_element_type=F32)


def _rms(x, g):
    ms = jnp.mean(x * x, axis=-1, keepdims=True)
    return x * lax.rsqrt(ms + EPS) * g


def _sigmoid(x):
    return 1.0 / (1.0 + jnp.exp(-x))


def _log_sigmoid(x):
    return jnp.minimum(x, 0.0) - jnp.log1p(jnp.exp(-jnp.abs(x)))


def _gelu_tanh(x):
    c = 0.7978845608028654
    return 0.5 * x * (1.0 + jnp.tanh(c * (x + 0.044715 * (x * x * x))))


def _const_spec(shape):
    nd = len(shape)
    return pl.BlockSpec(shape, lambda *_: (0,) * nd, pipeline_mode=pl.Buffered(1))


def _ffn_kernel(x_ref, g_ref, w1_ref, w3_ref, w2_ref, o_ref):
    x = x_ref[...]
    h = _rms(x, g_ref[...]).astype(BF16)
    a = _dot(h, w1_ref[...])
    b = _dot(h, w3_ref[...])
    u = (a * _sigmoid(a) * b).astype(BF16)
    o_ref[...] = x + FFN_RES * _dot(u, w2_ref[...])


def _ffn(x2, g, w1, w3, w2):
    t, d = x2.shape
    ff = w1.shape[1]
    tm = TOKEN_TILE
    return pl.pallas_call(
        _ffn_kernel,
        grid=(t // tm,),
        in_specs=[
            pl.BlockSpec((tm, d), lambda i: (i, 0)),
            _const_spec((1, d)),
            _const_spec((d, ff)),
            _const_spec((d, ff)),
            _const_spec((ff, d)),
        ],
        out_specs=pl.BlockSpec((tm, d), lambda i: (i, 0)),
        out_shape=jax.ShapeDtypeStruct((t, d), F32),
        compiler_params=pltpu.CompilerParams(
            dimension_semantics=("parallel",), vmem_limit_bytes=VMEM_LIMIT),
        name="half_ffn",
    )(x2, g, w1, w3, w2)


def _ple_kernel(x_ref, p_ref, g_ref, wg_ref, wp_ref, fn_ref, o_ref, *, final):
    x = x_ref[...]
    h = _rms(x, g_ref[...]).astype(BF16)
    gate = _sigmoid(_dot(h, wg_ref[...]))
    y = x + gate * _dot(p_ref[...].astype(BF16), wp_ref[...])
    if final:
        y = _rms(y, fn_ref[...])
    o_ref[...] = y


def _ple(x2, p2, g, wg, wp, fn, final):
    t, d = x2.shape
    pd = p2.shape[1]
    tm = TOKEN_TILE
    return pl.pallas_call(
        functools.partial(_ple_kernel, final=final),
        grid=(t // tm,),
        in_specs=[
            pl.BlockSpec((tm, d), lambda i: (i, 0)),
            pl.BlockSpec((tm, pd), lambda i: (i, 0)),
            _const_spec((1, d)),
            _const_spec((d, d)),
            _const_spec((pd, d)),
            _const_spec((1, d)),
        ],
        out_specs=pl.BlockSpec((tm, d), lambda i: (i, 0)),
        out_shape=jax.ShapeDtypeStruct((t, d), F32),
        compiler_params=pltpu.CompilerParams(
            dimension_semantics=("parallel",), vmem_limit_bytes=VMEM_LIMIT),
        name="ple_embed",
    )(x2, p2, g, wg, wp, fn)


def _chunk_suffix_prod(f):
    n = f.shape[0]
    row = lax.broadcasted_iota(jnp.int32, f.shape, 0) % CHUNK
    p = jnp.where(row < CHUNK - 1, pltpu.roll(f, n - 1, axis=0), 1.0)
    d = 1
    while d < CHUNK:
        p = p * jnp.where(row < CHUNK - d, pltpu.roll(p, n - d, axis=0), 1.0)
        d *= 2
    return p


def _chunk_state_attention(q, kdec, v, gall, s_ref, *, heads, dk, dv, heads_per_group):
    n = q.shape[0]
    groups = heads // heads_per_group
    gk = heads_per_group * dk
    gv = heads_per_group * dv
    if heads_per_group > 1:
        rv = lax.broadcasted_iota(jnp.int32, (gv, gk), 0) // dv
        ck = lax.broadcasted_iota(jnp.int32, (gv, gk), 1) // dk
        mask = rv == ck
    outs = []
    for gi in range(groups):
        s = s_ref[gi]
        cols_k = slice(gi * gk, (gi + 1) * gk)
        cols_v = slice(gi * gv, (gi + 1) * gv)
        o_rows = []
        for c in range(n // CHUNK):
            rows = slice(c * CHUNK, (c + 1) * CHUNK)
            kv = lax.dot_general(v[rows, cols_v], kdec[rows, cols_k],
                                 (((0,), (0,)), ((), ())), preferred_element_type=F32)
            if heads_per_group > 1:
                kv = jnp.where(mask, kv, 0.0)
            s = s * gall[c * CHUNK:c * CHUNK + 1, cols_k] + kv
            o_rows.append(lax.dot_general(q[rows, cols_k], s.astype(BF16),
                                          (((1,), (1,)), ((), ())), preferred_element_type=F32))
        s_ref[gi] = s
        outs.append(jnp.concatenate(o_rows, axis=0))
    return outs[0] if groups == 1 else jnp.concatenate(outs, axis=1)


def _head_rmsnorm(o, hnorm, heads, dv):
    parts = []
    for hh in range(heads):
        oh = o[:, hh * dv:(hh + 1) * dv]
        parts.append(_rms(oh, hnorm))
    return jnp.concatenate(parts, axis=1)


def _mix0_kernel(x_ref, nrm_ref, win_ref, gup_ref, gbias_ref, hnorm_ref, convw_ref, convb_ref,
                 wgate_ref, bgate_ref, lam_ref, wout_ref, o_ref,
                 s_ref, ext_ref, hc_ref, *, qk, vdim, lru, rank_pad):
    j = pl.program_id(1)
    ts = x_ref.shape[1]

    @pl.when(j == 0)
    def _():
        s_ref[...] = jnp.zeros_like(s_ref)
        ext_ref[0:SUBLANE, :] = jnp.zeros((SUBLANE, lru), F32)
        hc_ref[...] = jnp.zeros_like(hc_ref)

    x = x_ref[0]
    h = _rms(x, nrm_ref[...]).astype(BF16)

    c0 = 0
    q = _dot(h, win_ref[:, c0:c0 + qk]) * ((qk // GLA_HEADS) ** -0.5)
    c0 += qk
    k = _dot(h, win_ref[:, c0:c0 + qk])
    c0 += qk
    v = _dot(h, win_ref[:, c0:c0 + vdim])
    c0 += vdim
    g = _dot(h, win_ref[:, c0:c0 + vdim])
    c0 += vdim
    lr = _dot(h, win_ref[:, c0:c0 + rank_pad])
    c0 += rank_pad
    xr = _dot(h, win_ref[:, c0:c0 + lru])
    c0 += lru
    xg = _dot(h, win_ref[:, c0:c0 + lru])

    glog = _dot(lr.astype(BF16), gup_ref[...]) + gbias_ref[...]
    f = jnp.exp(_log_sigmoid(glog) * (1.0 / GLA_GATE_TEMP))
    dec = _chunk_suffix_prod(f)
    o = _chunk_state_attention(
        q.astype(BF16), (k * dec).astype(BF16), v.astype(BF16), f * dec, s_ref,
        heads=GLA_HEADS, dk=qk // GLA_HEADS, dv=vdim // GLA_HEADS, heads_per_group=GLA_HEADS)
    o = _head_rmsnorm(o, hnorm_ref[...], GLA_HEADS, vdim // GLA_HEADS) * (g * _sigmoid(g))

    ext_ref[SUBLANE:SUBLANE + ts, :] = xr
    xc = xr * convw_ref[CONV_W - 1:CONV_W, :] + convb_ref[...]
    for back in range(1, CONV_W):
        xc = xc + ext_ref[SUBLANE - back:SUBLANE - back + ts, :] * convw_ref[CONV_W - 1 - back:CONV_W - back, :]
    ext_ref[0:SUBLANE, :] = xr[ts - SUBLANE:ts, :]

    gates = _dot(xc.astype(BF16), wgate_ref[...]) + bgate_ref[...]
    r = _sigmoid(gates[:, :lru])
    i = _sigmoid(gates[:, lru:])
    a = jnp.exp((LRU_C * r) * _log_sigmoid(lam_ref[...]))
    b = jnp.sqrt(1.0 - a * a) * (i * xc)

    row = lax.broadcasted_iota(jnp.int32, a.shape, 0)
    d = 1
    while d < ts:
        keep = row >= d
        a_prev = jnp.where(keep, pltpu.roll(a, d, axis=0), 1.0)
        b_prev = jnp.where(keep, pltpu.roll(b, d, axis=0), 0.0)
        b = a * b_prev + b
        a = a * a_prev
        d *= 2
    hs = a * hc_ref[...] + b
    hc_ref[...] = hs[ts - 1:ts, :]
    y = hs * _gelu_tanh(xg)

    out = _dot(o.astype(BF16), wout_ref[0:vdim, :]) + _dot(y.astype(BF16), wout_ref[vdim:vdim + lru, :])
    o_ref[0] = x + out


def _mixer0(x3, nrm, win, gup, gbias, hnorm, convw, convb, wgate, bgate, lam, wout, *, qk, vdim, lru, rank_pad):
    b, s, d = x3.shape
    ts = SEQ_TILE
    cols = win.shape[1]
    kern = functools.partial(_mix0_kernel, qk=qk, vdim=vdim, lru=lru, rank_pad=rank_pad)
    return pl.pallas_call(
        kern,
        grid=(b, s // ts),
        in_specs=[
            pl.BlockSpec((1, ts, d), lambda bi, j: (bi, j, 0)),
            _const_spec((1, d)),
            _const_spec((d, cols)),
            _const_spec((rank_pad, qk)),
            _const_spec((1, qk)),
            _const_spec((1, vdim // GLA_HEADS)),
            _const_spec((CONV_W, lru)),
            _const_spec((1, lru)),
            _const_spec((lru, 2 * lru)),
            _const_spec((1, 2 * lru)),
            _const_spec((1, lru)),
            _const_spec((vdim + lru, d)),
        ],
        out_specs=pl.BlockSpec((1, ts, d), lambda bi, j: (bi, j, 0)),
        out_shape=jax.ShapeDtypeStruct((b, s, d), F32),
        scratch_shapes=[
            pltpu.VMEM((1, vdim, qk), F32),
            pltpu.VMEM((SUBLANE + ts, lru), F32),
            pltpu.VMEM((1, lru), F32),
        ],
        compiler_params=pltpu.CompilerParams(
            dimension_semantics=("parallel", "arbitrary"), vmem_limit_bytes=VMEM_LIMIT),
        name="mixer_gla_rglru",
    )(x3, nrm, win, gup, gbias, hnorm, convw, convb, wgate, bgate, lam, wout)


HG_HEADS_PER_GROUP = 2


def _mix1_kernel(x_ref, nrm_ref, win_ref, lbl_ref, hnorm_ref, wout_ref, o_ref, s_ref, *, layer, hk, hv):
    j = pl.program_id(1)

    @pl.when(j == 0)
    def _():
        s_ref[...] = jnp.zeros_like(s_ref)

    x = x_ref[0]
    h = _rms(x, nrm_ref[...]).astype(BF16)
    q = _dot(h, win_ref[:, 0:hk])
    fz = _dot(h, win_ref[:, hk:2 * hk])
    iv = _dot(h, win_ref[:, 2 * hk:2 * hk + hv])
    g = _dot(h, win_ref[:, 2 * hk + hv:2 * hk + 2 * hv])

    logits = lbl_ref[...]
    depth = logits.shape[0]
    m = jnp.max(logits, axis=0, keepdims=True)
    e = jnp.exp(logits - m)
    sm = e / jnp.sum(e, axis=0, keepdims=True)
    cs = sm[0:1, :]
    for li in range(1, layer + 1):
        cs = cs + sm[li:li + 1, :]
    lb = cs - sm[0:1, :]
    del depth

    sg = _sigmoid(fz)
    f = lb + (1.0 - lb) * sg
    k = (1.0 - lb) * (1.0 - sg)
    dec = _chunk_suffix_prod(f)
    heads = hk // HG_DK
    o = _chunk_state_attention(
        (q * _sigmoid(q)).astype(BF16), (k * dec).astype(BF16), iv.astype(BF16), f * dec, s_ref,
        heads=heads, dk=HG_DK, dv=HG_DV, heads_per_group=HG_HEADS_PER_GROUP)
    o = _head_rmsnorm(o, hnorm_ref[...], heads, HG_DV) * (g * _sigmoid(g))
    o_ref[0] = x + _dot(o.astype(BF16), wout_ref[...])


def _mixer1(x3, nrm, win, lbl, hnorm, wout, *, layer):
    b, s, d = x3.shape
    ts = SEQ_TILE
    cols = win.shape[1]
    hv = wout.shape[0]
    hk = (cols - 2 * hv) // 2
    heads = hk // HG_DK
    groups = heads // HG_HEADS_PER_GROUP
    kern = functools.partial(_mix1_kernel, layer=layer, hk=hk, hv=hv)
    return pl.pallas_call(
        kern,
        grid=(b, s // ts),
        in_specs=[
            pl.BlockSpec((1, ts, d), lambda bi, j: (bi, j, 0)),
            _const_spec((1, d)),
            _const_spec((d, cols)),
            _const_spec(lbl.shape),
            _const_spec((1, HG_DV)),
            _const_spec((hv, d)),
        ],
        out_specs=pl.BlockSpec((1, ts, d), lambda bi, j: (bi, j, 0)),
        out_shape=jax.ShapeDtypeStruct((b, s, d), F32),
        scratch_shapes=[
            pltpu.VMEM((groups, HG_HEADS_PER_GROUP * HG_DV, HG_HEADS_PER_GROUP * HG_DK), F32),
        ],
        compiler_params=pltpu.CompilerParams(
            dimension_semantics=("parallel", "arbitrary"), vmem_limit_bytes=VMEM_LIMIT),
        name="mixer_hgrn2",
    )(x3, nrm, win, lbl, hnorm, wout)


def _row(v):
    return v.reshape(1, -1).astype(F32)


def _block_diag_dense(w):
    g, n, _ = w.shape
    eye = jnp.eye(g, dtype=w.dtype)
    return (eye[:, None, :, None] * w[:, :, None, :]).reshape(g * n, g * n)


def kernel(x, p, ffn_norm, ffn_w1, ffn_w3, ffn_w2, mix_norm, ple_norm, ple_proj, ple_gate, final_norm,
           m0_w_in, gla_gate_up, gla_gate_bias, gla_head_norm, lru_conv_w, lru_conv_b, lru_wa, lru_ba,
           lru_wx, lru_bx, lru_lambda, m0_w_out, m1_w_in, hgrn_lb_logits, hgrn_head_norm, m1_w_out):
    bsz, seq, d = x.shape
    depth = p.shape[0]
    t = bsz * seq
    x2 = x.reshape(t, d)

    qk = gla_gate_up.shape[2]
    rank = gla_gate_up.shape[1]
    lru = lru_lambda.shape[1]
    vdim = m0_w_out.shape[1] - lru
    rank_pad = LANE

    for layer in range(depth):
        x2 = _ffn(x2, _row(ffn_norm[layer, 0]), ffn_w1[layer, 0].astype(BF16),
                  ffn_w3[layer, 0].astype(BF16), ffn_w2[layer, 0].astype(BF16))
        if layer % 2 == 0:
            e = layer // 2
            w = m0_w_in[e]
            split = 2 * qk + 2 * vdim
            win = jnp.concatenate([
                w[:, :split],
                jnp.pad(w[:, split:split + rank], ((0, 0), (0, rank_pad - rank))),
                w[:, split + rank:],
            ], axis=1).astype(BF16)
            gup = jnp.pad(gla_gate_up[e], ((0, rank_pad - rank), (0, 0))).astype(BF16)
            wgate = jnp.concatenate([_block_diag_dense(lru_wa[e]), _block_diag_dense(lru_wx[e])],
                                    axis=1).astype(BF16)
            bgate = jnp.concatenate([lru_ba[e], lru_bx[e]]).reshape(1, -1)
            x3 = _mixer0(x2.reshape(bsz, seq, d), _row(mix_norm[layer]), win, gup, _row(gla_gate_bias[e]),
                         _row(gla_head_norm[e]), lru_conv_w[e], _row(lru_conv_b[e]), wgate, bgate,
                         _row(lru_lambda[e]), m0_w_out[e].astype(BF16),
                         qk=qk, vdim=vdim, lru=lru, rank_pad=rank_pad)
        else:
            o = layer // 2
            x3 = _mixer1(x2.reshape(bsz, seq, d), _row(mix_norm[layer]), m1_w_in[o].astype(BF16),
                         hgrn_lb_logits, _row(hgrn_head_norm[o]), m1_w_out[o].astype(BF16), layer=layer)
        x2 = x3.reshape(t, d)
        x2 = _ffn(x2, _row(ffn_norm[layer, 1]), ffn_w1[layer, 1].astype(BF16),
                  ffn_w3[layer, 1].astype(BF16), ffn_w2[layer, 1].astype(BF16))
        x2 = _ple(x2, p[layer].reshape(t, -1), _row(ple_norm[layer]), ple_gate[layer].astype(BF16),
                  ple_proj[layer].astype(BF16), _row(final_norm), final=(layer == depth - 1))
    return x2.reshape(bsz, seq, d)
```

```python
import functools

import jax
import jax.numpy as jnp
from jax import lax
from jax.experimental import pallas as pl
from jax.experimental.pallas import tpu as pltpu

F32 = jnp.float32
BF16 = jnp.bfloat16

EPS = 1e-6
CHUNK = 64
FFN_RES = 0.5
GLA_HEADS = 4
GLA_GATE_TEMP = 16.0
LRU_C = 8.0
CONV_W = 4
HG_DK = 128
HG_DV = 128

LANE = 128
SUBLANE = 8
VMEM_LIMIT = 56 * 1024 * 1024

TOKEN_TILE = 512
SEQ_TILE = 512
SUBTILES = 1


def _dot(a, b):
    return jnp.dot(a, b, preferred_element_type=F32)


def _rms(x, g):
    ms = jnp.mean(x * x, axis=-1, keepdims=True)
    return x * lax.rsqrt(ms + EPS) * g


def _sigmoid(x):
    return 1.0 / (1.0 + jnp.exp(-x))


def _log_sigmoid(x):
    return jnp.minimum(x, 0.0) - jnp.log1p(jnp.exp(-jnp.abs(x)))


def _gelu_tanh(x):
    c = 0.7978845608028654
    return 0.5 * x * (1.0 + jnp.tanh(c * (x + 0.044715 * (x * x * x))))


def _const_spec(shape):
    nd = len(shape)
    return pl.BlockSpec(shape, lambda *_: (0,) * nd, pipeline_mode=pl.Buffered(1))


def _ffn_kernel(x_ref, g_ref, w1_ref, w3_ref, w2_ref, o_ref):
    x = x_ref[...]
    h = _rms(x, g_ref[...]).astype(BF16)
    a = _dot(h, w1_ref[...])
    b = _dot(h, w3_ref[...])
    u = (a * _sigmoid(a) * b).astype(BF16)
    o_ref[...] = x + FFN_RES * _dot(u, w2_ref[...])


def _ffn(x2, g, w1, w3, w2):
    t, d = x2.shape
    ff = w1.shape[1]
    tm = TOKEN_TILE
    return pl.pallas_call(
        _ffn_kernel,
        grid=(t // tm,),
        in_specs=[
            pl.BlockSpec((tm, d), lambda i: (i, 0)),
            _const_spec((1, d)),
            _const_spec((d, ff)),
            _const_spec((d, ff)),
            _const_spec((ff, d)),
        ],
        out_specs=pl.BlockSpec((tm, d), lambda i: (i, 0)),
        out_shape=jax.ShapeDtypeStruct((t, d), F32),
        compiler_params=pltpu.CompilerParams(
            dimension_semantics=("parallel",), vmem_limit_bytes=VMEM_LIMIT),
        name="half_ffn",
    )(x2, g, w1, w3, w2)


def _ple_kernel(x_ref, p_ref, g_ref, wg_ref, wp_ref, fn_ref, o_ref, *, final):
    x = x_ref[...]
    h = _rms(x, g_ref[...]).astype(BF16)
    gate = _sigmoid(_dot(h, wg_ref[...]))
    y = x + gate * _dot(p_ref[...].astype(BF16), wp_ref[...])
    if final:
        y = _rms(y, fn_ref[...])
    o_ref[...] = y


def _ple(x2, p2, g, wg, wp, fn, final):
    t, d = x2.shape
    pd = p2.shape[1]
    tm = TOKEN_TILE
    return pl.pallas_call(
        functools.partial(_ple_kernel, final=final),
        grid=(t // tm,),
        in_specs=[
            pl.BlockSpec((tm, d), lambda i: (i, 0)),
            pl.BlockSpec((tm, pd), lambda i: (i, 0)),
            _const_spec((1, d)),
            _const_spec((d, d)),
            _const_spec((pd, d)),
            _const_spec((1, d)),
        ],
        out_specs=pl.BlockSpec((tm, d), lambda i: (i, 0)),
        out_shape=jax.ShapeDtypeStruct((t, d), F32),
        compiler_params=pltpu.CompilerParams(
            dimension_semantics=("parallel",), vmem_limit_bytes=VMEM_LIMIT),
        name="ple_embed",
    )(x2, p2, g, wg, wp, fn)


def _chunk_suffix_prod(f):
    n, c = f.shape
    groups = CHUNK // SUBLANE
    f4 = f.reshape(n // CHUNK, groups, SUBLANE, c)
    sub = lax.broadcasted_iota(jnp.int32, f4.shape, 2)
    p = jnp.where(sub < SUBLANE - 1, pltpu.roll(f4, SUBLANE - 1, axis=2), 1.0)
    d = 1
    while d < SUBLANE:
        p = p * jnp.where(sub < SUBLANE - d, pltpu.roll(p, SUBLANE - d, axis=2), 1.0)
        d *= 2
    tot = jnp.broadcast_to(f4[:, :, 0:1, :] * p[:, :, 0:1, :], f4.shape)
    parts = [None] * groups
    parts[groups - 1] = p[:, groups - 1:groups]
    later = tot[:, groups - 1:groups]
    for gi in range(groups - 2, -1, -1):
        parts[gi] = p[:, gi:gi + 1] * later
        if gi > 0:
            later = later * tot[:, gi:gi + 1]
    return jnp.concatenate(parts, axis=1).reshape(n, c)


def _chunk_state_attention(q, kdec, v, gall, s_ref, *, heads, dk, dv, heads_per_group):
    n = q.shape[0]
    nchunks = n // CHUNK
    groups = heads // heads_per_group
    gk = heads_per_group * dk
    gv = heads_per_group * dv
    if heads_per_group > 1:
        rv = lax.broadcasted_iota(jnp.int32, (gv, gk), 0) // dv
        ck = lax.broadcasted_iota(jnp.int32, (gv, gk), 1) // dk
        mask = rv == ck

    def rows(c):
        return slice(c * CHUNK, (c + 1) * CHUNK)

    def cols(gi, width):
        return slice(gi * width, (gi + 1) * width)

    kvs = [[lax.dot_general(v[rows(c), cols(gi, gv)], kdec[rows(c), cols(gi, gk)],
                            (((0,), (0,)), ((), ())), preferred_element_type=F32)
            for gi in range(groups)] for c in range(nchunks)]
    states = [[None] * groups for _ in range(nchunks)]
    for gi in range(groups):
        s = s_ref[gi]
        for c in range(nchunks):
            kv = kvs[c][gi]
            if heads_per_group > 1:
                kv = jnp.where(mask, kv, 0.0)
            s = s * gall[c * CHUNK:c * CHUNK + 1, cols(gi, gk)] + kv
            states[c][gi] = s.astype(BF16)
        s_ref[gi] = s
    o_rows = []
    for c in range(nchunks):
        parts = [lax.dot_general(q[rows(c), cols(gi, gk)], states[c][gi],
                                 (((1,), (1,)), ((), ())), preferred_element_type=F32)
                 for gi in range(groups)]
        o_rows.append(parts[0] if groups == 1 else jnp.concatenate(parts, axis=1))
    return jnp.concatenate(o_rows, axis=0)


def _linear_scan(a, b, carry_ref):
    n, c = a.shape
    groups = n // SUBLANE
    a3 = a.reshape(groups, SUBLANE, c)
    b3 = b.reshape(groups, SUBLANE, c)
    sub = lax.broadcasted_iota(jnp.int32, a3.shape, 1)
    d = 1
    while d < SUBLANE:
        keep = sub >= d
        a_prev = jnp.where(keep, pltpu.roll(a3, d, axis=1), 1.0)
        b_prev = jnp.where(keep, pltpu.roll(b3, d, axis=1), 0.0)
        b3 = a3 * b_prev + b3
        a3 = a3 * a_prev
        d *= 2
    h = jnp.broadcast_to(carry_ref[...], (SUBLANE, c))
    out = []
    for gi in range(groups):
        hg = a3[gi] * h + b3[gi]
        out.append(hg)
        h = jnp.broadcast_to(hg[SUBLANE - 1:SUBLANE, :], (SUBLANE, c))
    carry_ref[...] = h[0:1, :]
    return jnp.concatenate(out, axis=0)


def _head_rmsnorm(o, hnorm, heads, dv):
    parts = []
    for hh in range(heads):
        oh = o[:, hh * dv:(hh + 1) * dv]
        parts.append(_rms(oh, hnorm))
    return jnp.concatenate(parts, axis=1)


GLA_HEADS_PER_GROUP = 4


def _mix0_kernel(x_ref, nrm_ref, win_ref, gup_ref, gbias_ref, hnorm_ref, convw_ref, convb_ref,
                 wgate_ref, bgate_ref, lam_ref, wout_ref, o_ref,
                 s_ref, ext_ref, hc_ref, *, qk, vdim, lru, rank_pad):
    j = pl.program_id(1)
    ts = x_ref.shape[1]

    @pl.when(j == 0)
    def _():
        s_ref[...] = jnp.zeros_like(s_ref)
        ext_ref[0:SUBLANE, :] = jnp.zeros((SUBLANE, lru), F32)
        hc_ref[...] = jnp.zeros_like(hc_ref)

    x = x_ref[0]
    h = _rms(x, nrm_ref[...]).astype(BF16)
    tsub = ts // SUBTILES
    zs = [_dot(h[r0:r0 + tsub], win_ref[...]) for r0 in range(0, ts, tsub)]

    o_k, o_v, o_g, o_lr = qk, 2 * qk, 2 * qk + vdim, 2 * qk + 2 * vdim
    o_xr = o_lr + rank_pad
    o_xg = o_xr + lru
    log_lam = _log_sigmoid(lam_ref[...])
    for si, r0 in enumerate(range(0, ts, tsub)):
        z = zs[si]
        q = z[:, 0:qk] * ((qk // GLA_HEADS) ** -0.5)
        k = z[:, o_k:o_k + qk]
        v = z[:, o_v:o_v + vdim]
        g = z[:, o_g:o_g + vdim]
        lr = z[:, o_lr:o_lr + rank_pad]
        xr = z[:, o_xr:o_xr + lru]
        xg = z[:, o_xg:o_xg + lru]

        glog = _dot(lr.astype(BF16), gup_ref[...]) + gbias_ref[...]
        f = jnp.exp(_log_sigmoid(glog) * (1.0 / GLA_GATE_TEMP))
        dec = _chunk_suffix_prod(f)
        o = _chunk_state_attention(
            q.astype(BF16), (k * dec).astype(BF16), v.astype(BF16), f * dec, s_ref,
            heads=GLA_HEADS, dk=qk // GLA_HEADS, dv=vdim // GLA_HEADS,
            heads_per_group=GLA_HEADS_PER_GROUP)
        o = _head_rmsnorm(o, hnorm_ref[...], GLA_HEADS, vdim // GLA_HEADS) * (g * _sigmoid(g))

        e0 = SUBLANE + r0
        ext_ref[e0:e0 + tsub, :] = xr
        xc = xr * convw_ref[CONV_W - 1:CONV_W, :] + convb_ref[...]
        for back in range(1, CONV_W):
            xc = xc + ext_ref[e0 - back:e0 - back + tsub, :] * convw_ref[CONV_W - 1 - back:CONV_W - back, :]

        gates = _dot(xc.astype(BF16), wgate_ref[...]) + bgate_ref[...]
        r = _sigmoid(gates[:, :lru])
        i = _sigmoid(gates[:, lru:])
        a = jnp.exp((LRU_C * r) * log_lam)
        t = 1.0 - a * a
        b = jnp.where(t > 0.0, t * lax.rsqrt(t), 0.0) * (i * xc)
        hs = _linear_scan(a, b, hc_ref)
        y = hs * _gelu_tanh(xg)

        out = (_dot(o.astype(BF16), wout_ref[0:vdim, :])
               + _dot(y.astype(BF16), wout_ref[vdim:vdim + lru, :]))
        o_ref[0, r0:r0 + tsub, :] = x[r0:r0 + tsub] + out
    ext_ref[0:SUBLANE, :] = ext_ref[ts:ts + SUBLANE, :]


def _mixer0(x3, nrm, win, gup, gbias, hnorm, convw, convb, wgate, bgate, lam, wout, *, qk, vdim, lru, rank_pad):
    b, s, d = x3.shape
    ts = SEQ_TILE
    cols = win.shape[1]
    kern = functools.partial(_mix0_kernel, qk=qk, vdim=vdim, lru=lru, rank_pad=rank_pad)
    return pl.pallas_call(
        kern,
        grid=(b, s // ts),
        in_specs=[
            pl.BlockSpec((1, ts, d), lambda bi, j: (bi, j, 0)),
            _const_spec((1, d)),
            _const_spec((d, cols)),
            _const_spec((rank_pad, qk)),
            _const_spec((1, qk)),
            _const_spec((1, vdim // GLA_HEADS)),
            _const_spec((CONV_W, lru)),
            _const_spec((1, lru)),
            _const_spec((lru, 2 * lru)),
            _const_spec((1, 2 * lru)),
            _const_spec((1, lru)),
            _const_spec((vdim + lru, d)),
        ],
        out_specs=pl.BlockSpec((1, ts, d), lambda bi, j: (bi, j, 0)),
        out_shape=jax.ShapeDtypeStruct((b, s, d), F32),
        scratch_shapes=[
            pltpu.VMEM((GLA_HEADS // GLA_HEADS_PER_GROUP, GLA_HEADS_PER_GROUP * (vdim // GLA_HEADS),
                        GLA_HEADS_PER_GROUP * (qk // GLA_HEADS)), F32),
            pltpu.VMEM((SUBLANE + ts, lru), F32),
            pltpu.VMEM((1, lru), F32),
        ],
        compiler_params=pltpu.CompilerParams(
            dimension_semantics=("parallel", "arbitrary"), vmem_limit_bytes=VMEM_LIMIT),
        name="mixer_gla_rglru",
    )(x3, nrm, win, gup, gbias, hnorm, convw, convb, wgate, bgate, lam, wout)


HG_HEADS_PER_GROUP = 2


def _mix1_kernel(x_ref, nrm_ref, win_ref, lbl_ref, hnorm_ref, wout_ref, o_ref, s_ref, *, layer, hk, hv):
    j = pl.program_id(1)

    @pl.when(j == 0)
    def _():
        s_ref[...] = jnp.zeros_like(s_ref)

    ts = x_ref.shape[1]
    tsub = ts // SUBTILES
    x = x_ref[0]
    h = _rms(x, nrm_ref[...]).astype(BF16)
    zs = [_dot(h[r0:r0 + tsub], win_ref[...]) for r0 in range(0, ts, tsub)]

    logits = lbl_ref[...]
    m = jnp.max(logits, axis=0, keepdims=True)
    e = jnp.exp(logits - m)
    sm = e / jnp.sum(e, axis=0, keepdims=True)
    cs = sm[0:1, :]
    for li in range(1, layer + 1):
        cs = cs + sm[li:li + 1, :]
    lb = cs - sm[0:1, :]

    heads = hk // HG_DK
    for si, r0 in enumerate(range(0, ts, tsub)):
        z = zs[si]
        q = z[:, 0:hk]
        fz = z[:, hk:2 * hk]
        iv = z[:, 2 * hk:2 * hk + hv]
        g = z[:, 2 * hk + hv:2 * hk + 2 * hv]
        sg = _sigmoid(fz)
        f = lb + (1.0 - lb) * sg
        k = (1.0 - lb) * (1.0 - sg)
        dec = _chunk_suffix_prod(f)
        o = _chunk_state_attention(
            (q * _sigmoid(q)).astype(BF16), (k * dec).astype(BF16), iv.astype(BF16), f * dec, s_ref,
            heads=heads, dk=HG_DK, dv=HG_DV, heads_per_group=HG_HEADS_PER_GROUP)
        o = _head_rmsnorm(o, hnorm_ref[...], heads, HG_DV) * (g * _sigmoid(g))
        o_ref[0, r0:r0 + tsub, :] = x[r0:r0 + tsub] + _dot(o.astype(BF16), wout_ref[...])


def _mixer1(x3, nrm, win, lbl, hnorm, wout, *, layer):
    b, s, d = x3.shape
    ts = SEQ_TILE
    cols = win.shape[1]
    hv = wout.shape[0]
    hk = (cols - 2 * hv) // 2
    heads = hk // HG_DK
    groups = heads // HG_HEADS_PER_GROUP
    kern = functools.partial(_mix1_kernel, layer=layer, hk=hk, hv=hv)
    return pl.pallas_call(
        kern,
        grid=(b, s // ts),
        in_specs=[
            pl.BlockSpec((1, ts, d), lambda bi, j: (bi, j, 0)),
            _const_spec((1, d)),
            _const_spec((d, cols)),
            _const_spec(lbl.shape),
            _const_spec((1, HG_DV)),
            _const_spec((hv, d)),
        ],
        out_specs=pl.BlockSpec((1, ts, d), lambda bi, j: (bi, j, 0)),
        out_shape=jax.ShapeDtypeStruct((b, s, d), F32),
        scratch_shapes=[
            pltpu.VMEM((groups, HG_HEADS_PER_GROUP * HG_DV, HG_HEADS_PER_GROUP * HG_DK), F32),
        ],
        compiler_params=pltpu.CompilerParams(
            dimension_semantics=("parallel", "arbitrary"), vmem_limit_bytes=VMEM_LIMIT),
        name="mixer_hgrn2",
    )(x3, nrm, win, lbl, hnorm, wout)


def _row(v):
    return v.reshape(1, -1).astype(F32)


def _block_diag_dense(w):
    g, n, _ = w.shape
    eye = jnp.eye(g, dtype=w.dtype)
    return (eye[:, None, :, None] * w[:, :, None, :]).reshape(g * n, g * n)


def kernel(x, p, ffn_norm, ffn_w1, ffn_w3, ffn_w2, mix_norm, ple_norm, ple_proj, ple_gate, final_norm,
           m0_w_in, gla_gate_up, gla_gate_bias, gla_head_norm, lru_conv_w, lru_conv_b, lru_wa, lru_ba,
           lru_wx, lru_bx, lru_lambda, m0_w_out, m1_w_in, hgrn_lb_logits, hgrn_head_norm, m1_w_out):
    bsz, seq, d = x.shape
    depth = p.shape[0]
    t = bsz * seq
    x2 = x.reshape(t, d)

    qk = gla_gate_up.shape[2]
    rank = gla_gate_up.shape[1]
    lru = lru_lambda.shape[1]
    vdim = m0_w_out.shape[1] - lru
    rank_pad = LANE

    for layer in range(depth):
        x2 = _ffn(x2, _row(ffn_norm[layer, 0]), ffn_w1[layer, 0].astype(BF16),
                  ffn_w3[layer, 0].astype(BF16), ffn_w2[layer, 0].astype(BF16))
        if layer % 2 == 0:
            e = layer // 2
            w = m0_w_in[e]
            split = 2 * qk + 2 * vdim
            win = jnp.concatenate([
                w[:, :split],
                jnp.pad(w[:, split:split + rank], ((0, 0), (0, rank_pad - rank))),
                w[:, split + rank:],
            ], axis=1).astype(BF16)
            gup = jnp.pad(gla_gate_up[e], ((0, rank_pad - rank), (0, 0))).astype(BF16)
            wgate = jnp.concatenate([_block_diag_dense(lru_wa[e]), _block_diag_dense(lru_wx[e])],
                                    axis=1).astype(BF16)
            bgate = jnp.concatenate([lru_ba[e], lru_bx[e]]).reshape(1, -1)
            x3 = _mixer0(x2.reshape(bsz, seq, d), _row(mix_norm[layer]), win, gup, _row(gla_gate_bias[e]),
                         _row(gla_head_norm[e]), lru_conv_w[e], _row(lru_conv_b[e]), wgate, bgate,
                         _row(lru_lambda[e]), m0_w_out[e].astype(BF16),
                         qk=qk, vdim=vdim, lru=lru, rank_pad=rank_pad)
        else:
            o = layer // 2
            x3 = _mixer1(x2.reshape(bsz, seq, d), _row(mix_norm[layer]), m1_w_in[o].astype(BF16),
                         hgrn_lb_logits, _row(hgrn_head_norm[o]), m1_w_out[o].astype(BF16), layer=layer)
        x2 = x3.reshape(t, d)
        x2 = _ffn(x2, _row(ffn_norm[layer, 1]), ffn_w1[layer, 1].astype(BF16),
                  ffn_w3[layer, 1].astype(BF16), ffn_w2[layer, 1].astype(BF16))
        x2 = _ple(x2, p[layer].reshape(t, -1), _row(ple_norm[layer]), ple_gate[layer].astype(BF16),
                  ple_proj[layer].astype(BF16), _row(final_norm), final=(layer == depth - 1))
    return x2.reshape(bsz, seq, d)
```

```python
import functools

import jax
import jax.numpy as jnp
from jax import lax
from jax.experimental import pallas as pl
from jax.experimental.pallas import tpu as pltpu

F32 = jnp.float32
BF16 = jnp.bfloat16

EPS = 1e-6
CHUNK = 64
FFN_RES = 0.5
GLA_HEADS = 4
GLA_GATE_TEMP = 16.0
LRU_C = 8.0
CONV_W = 4
HG_DK = 128
HG_DV = 128

LANE = 128
SUBLANE = 8
VMEM_LIMIT = 56 * 1024 * 1024

TOKEN_TILE = 512
FFN_SUBTILES = 2
SEQ_TILE = 512
SUBTILES = 1


def _dot(a, b):
    return jnp.dot(a, b, preferred_element_type=F32)


def _rms(x, g):
    ms = jnp.mean(x * x, axis=-1, keepdims=True)
    return x * lax.rsqrt(ms + EPS) * g


def _sigmoid(x):
    return 1.0 / (1.0 + jnp.exp(-x))


def _log_sigmoid(x):
    return jnp.minimum(x, 0.0) - jnp.log1p(jnp.exp(-jnp.abs(x)))


def _gelu_tanh(x):
    c = 0.7978845608028654
    return 0.5 * x * (1.0 + jnp.tanh(c * (x + 0.044715 * (x * x * x))))


def _const_spec(shape):
    nd = len(shape)
    return pl.BlockSpec(shape, lambda *_: (0,) * nd, pipeline_mode=pl.Buffered(1))


def _ffn_kernel(*refs, ple, final):
    if ple:
        x_ref, g_ref, w1_ref, w3_ref, w2_ref, p_ref, pg_ref, wg_ref, wp_ref, fn_ref, o_ref = refs
    else:
        x_ref, g_ref, w1_ref, w3_ref, w2_ref, o_ref = refs
    tm = x_ref.shape[0]
    tsub = tm // FFN_SUBTILES
    starts = range(0, tm, tsub)
    xs = [x_ref[r0:r0 + tsub, :] for r0 in starts]
    hs = [_rms(x, g_ref[...]).astype(BF16) for x in xs]
    ab = [(_dot(h, w1_ref[...]), _dot(h, w3_ref[...])) for h in hs]
    for r0, x, (a, b) in zip(starts, xs, ab):
        u = (a * _sigmoid(a) * b).astype(BF16)
        y = x + FFN_RES * _dot(u, w2_ref[...])
        if ple:
            gate = _sigmoid(_dot(_rms(y, pg_ref[...]).astype(BF16), wg_ref[...]))
            y = y + gate * _dot(p_ref[r0:r0 + tsub, :].astype(BF16), wp_ref[...])
            if final:
                y = _rms(y, fn_ref[...])
        o_ref[r0:r0 + tsub, :] = y


def _stacked_spec(shape, index):
    nd = len(shape)
    return pl.BlockSpec((None,) + tuple(shape[1:]), lambda *_: (index,) + (0,) * (nd - 1),
                        pipeline_mode=pl.Buffered(1))


def _ffn(x2, norms, w1, w3, w2, widx, ple_args=None):
    t, d = x2.shape
    tm = TOKEN_TILE
    in_specs = [
        pl.BlockSpec((tm, d), lambda i: (i, 0)),
        _stacked_spec(norms.shape, widx),
        _stacked_spec(w1.shape, widx),
        _stacked_spec(w3.shape, widx),
        _stacked_spec(w2.shape, widx),
    ]
    args = [x2, norms, w1, w3, w2]
    ple = ple_args is not None
    final = False
    if ple:
        p, pnorm, pgate, pproj, fnorm, layer, final = ple_args
        pd = p.shape[2]
        in_specs += [
            pl.BlockSpec((None, tm, pd), lambda i: (layer, i, 0)),
            _stacked_spec(pnorm.shape, layer),
            _stacked_spec(pgate.shape, layer),
            _stacked_spec(pproj.shape, layer),
            _const_spec(fnorm.shape),
        ]
        args += [p, pnorm, pgate, pproj, fnorm]
    return pl.pallas_call(
        functools.partial(_ffn_kernel, ple=ple, final=final),
        grid=(t // tm,),
        in_specs=in_specs,
        out_specs=pl.BlockSpec((tm, d), lambda i: (i, 0)),
        out_shape=jax.ShapeDtypeStruct((t, d), F32),
        compiler_params=pltpu.CompilerParams(
            dimension_semantics=("parallel",), vmem_limit_bytes=VMEM_LIMIT),
        name="half_ffn_ple" if ple else "half_ffn",
    )(*args)


def _chunk_suffix_prod(f):
    n, c = f.shape
    groups = CHUNK // SUBLANE
    f4 = f.reshape(n // CHUNK, groups, SUBLANE, c)
    sub = lax.broadcasted_iota(jnp.int32, f4.shape, 2)
    p = jnp.where(sub < SUBLANE - 1, pltpu.roll(f4, SUBLANE - 1, axis=2), 1.0)
    d = 1
    while d < SUBLANE:
        p = p * jnp.where(sub < SUBLANE - d, pltpu.roll(p, SUBLANE - d, axis=2), 1.0)
        d *= 2
    tot = jnp.broadcast_to(f4[:, :, 0:1, :] * p[:, :, 0:1, :], f4.shape)
    parts = [None] * groups
    parts[groups - 1] = p[:, groups - 1:groups]
    later = tot[:, groups - 1:groups]
    for gi in range(groups - 2, -1, -1):
        parts[gi] = p[:, gi:gi + 1] * later
        if gi > 0:
            later = later * tot[:, gi:gi + 1]
    return jnp.concatenate(parts, axis=1).reshape(n, c)


def _chunk_state_attention(q, kdec, v, gall, s_ref, *, heads, dk, dv, heads_per_group):
    n = q.shape[0]
    nchunks = n // CHUNK
    groups = heads // heads_per_group
    gk = heads_per_group * dk
    gv = heads_per_group * dv
    if heads_per_group > 1:
        rv = lax.broadcasted_iota(jnp.int32, (gv, gk), 0) // dv
        ck = lax.broadcasted_iota(jnp.int32, (gv, gk), 1) // dk
        mask = rv == ck

    def rows(c):
        return slice(c * CHUNK, (c + 1) * CHUNK)

    def cols(gi, width):
        return slice(gi * width, (gi + 1) * width)

    kvs = [[lax.dot_general(v[rows(c), cols(gi, gv)], kdec[rows(c), cols(gi, gk)],
                            (((0,), (0,)), ((), ())), preferred_element_type=F32)
            for gi in range(groups)] for c in range(nchunks)]
    states = [[None] * groups for _ in range(nchunks)]
    for gi in range(groups):
        s = s_ref[gi]
        for c in range(nchunks):
            kv = kvs[c][gi]
            if heads_per_group > 1:
                kv = jnp.where(mask, kv, 0.0)
            s = s * gall[c * CHUNK:c * CHUNK + 1, cols(gi, gk)] + kv
            states[c][gi] = s.astype(BF16)
        s_ref[gi] = s
    o_rows = []
    for c in range(nchunks):
        parts = [lax.dot_general(q[rows(c), cols(gi, gk)], states[c][gi],
                                 (((1,), (1,)), ((), ())), preferred_element_type=F32)
                 for gi in range(groups)]
        o_rows.append(parts[0] if groups == 1 else jnp.concatenate(parts, axis=1))
    return jnp.concatenate(o_rows, axis=0)


def _linear_scan(a, b, carry_ref):
    n, c = a.shape
    groups = n // SUBLANE
    a3 = a.reshape(groups, SUBLANE, c)
    b3 = b.reshape(groups, SUBLANE, c)
    sub = lax.broadcasted_iota(jnp.int32, a3.shape, 1)
    d = 1
    while d < SUBLANE:
        keep = sub >= d
        a_prev = jnp.where(keep, pltpu.roll(a3, d, axis=1), 1.0)
        b_prev = jnp.where(keep, pltpu.roll(b3, d, axis=1), 0.0)
        b3 = a3 * b_prev + b3
        a3 = a3 * a_prev
        d *= 2
    h = jnp.broadcast_to(carry_ref[...], (SUBLANE, c))
    out = []
    for gi in range(groups):
        hg = a3[gi] * h + b3[gi]
        out.append(hg)
        h = jnp.broadcast_to(hg[SUBLANE - 1:SUBLANE, :], (SUBLANE, c))
    carry_ref[...] = h[0:1, :]
    return jnp.concatenate(out, axis=0)


def _head_rmsnorm(o, hnorm, heads, dv):
    parts = []
    for hh in range(heads):
        oh = o[:, hh * dv:(hh + 1) * dv]
        parts.append(_rms(oh, hnorm))
    return jnp.concatenate(parts, axis=1)


GLA_HEADS_PER_GROUP = 4


def _mix0_kernel(x_ref, nrm_ref, win_ref, gup_ref, gbias_ref, hnorm_ref, convw_ref, convb_ref,
                 wgate_ref, bgate_ref, lam_ref, wout_ref, o_ref,
                 s_ref, ext_ref, hc_ref, *, qk, vdim, lru, rank_pad):
    j = pl.program_id(1)
    ts = x_ref.shape[1]

    @pl.when(j == 0)
    def _():
        s_ref[...] = jnp.zeros_like(s_ref)
        ext_ref[0:SUBLANE, :] = jnp.zeros((SUBLANE, lru), F32)
        hc_ref[...] = jnp.zeros_like(hc_ref)

    x = x_ref[0]
    h = _rms(x, nrm_ref[...]).astype(BF16)
    tsub = ts // SUBTILES
    zs = [_dot(h[r0:r0 + tsub], win_ref[...]) for r0 in range(0, ts, tsub)]

    o_k, o_v, o_g, o_lr = qk, 2 * qk, 2 * qk + vdim, 2 * qk + 2 * vdim
    o_xr = o_lr + rank_pad
    o_xg = o_xr + lru
    log_lam = _log_sigmoid(lam_ref[...])
    for si, r0 in enumerate(range(0, ts, tsub)):
        z = zs[si]
        q = z[:, 0:qk] * ((qk // GLA_HEADS) ** -0.5)
        k = z[:, o_k:o_k + qk]
        v = z[:, o_v:o_v + vdim]
        g = z[:, o_g:o_g + vdim]
        lr = z[:, o_lr:o_lr + rank_pad]
        xr = z[:, o_xr:o_xr + lru]
        xg = z[:, o_xg:o_xg + lru]

        glog = _dot(lr.astype(BF16), gup_ref[...]) + gbias_ref[...]
        f = jnp.exp(_log_sigmoid(glog) * (1.0 / GLA_GATE_TEMP))
        dec = _chunk_suffix_prod(f)
        o = _chunk_state_attention(
            q.astype(BF16), (k * dec).astype(BF16), v.astype(BF16), f * dec, s_ref,
            heads=GLA_HEADS, dk=qk // GLA_HEADS, dv=vdim // GLA_HEADS,
            heads_per_group=GLA_HEADS_PER_GROUP)
        o = _head_rmsnorm(o, hnorm_ref[...], GLA_HEADS, vdim // GLA_HEADS) * (g * _sigmoid(g))

        e0 = SUBLANE + r0
        ext_ref[e0:e0 + tsub, :] = xr
        xc = xr * convw_ref[CONV_W - 1:CONV_W, :] + convb_ref[...]
        for back in range(1, CONV_W):
            xc = xc + ext_ref[e0 - back:e0 - back + tsub, :] * convw_ref[CONV_W - 1 - back:CONV_W - back, :]

        gates = _dot(xc.astype(BF16), wgate_ref[...]) + bgate_ref[...]
        r = _sigmoid(gates[:, :lru])
        i = _sigmoid(gates[:, lru:])
        a = jnp.exp((LRU_C * r) * log_lam)
        t = 1.0 - a * a
        b = jnp.where(t > 0.0, t * lax.rsqrt(t), 0.0) * (i * xc)
        hs = _linear_scan(a, b, hc_ref)
        y = hs * _gelu_tanh(xg)

        out = (_dot(o.astype(BF16), wout_ref[0:vdim, :])
               + _dot(y.astype(BF16), wout_ref[vdim:vdim + lru, :]))
        o_ref[0, r0:r0 + tsub, :] = x[r0:r0 + tsub] + out
    ext_ref[0:SUBLANE, :] = ext_ref[ts:ts + SUBLANE, :]


def _mixer0(x3, nrm, win, gup, gbias, hnorm, convw, convb, wgate, bgate, lam, wout, *, qk, vdim, lru, rank_pad):
    b, s, d = x3.shape
    ts = SEQ_TILE
    cols = win.shape[1]
    kern = functools.partial(_mix0_kernel, qk=qk, vdim=vdim, lru=lru, rank_pad=rank_pad)
    return pl.pallas_call(
        kern,
        grid=(b, s // ts),
        in_specs=[
            pl.BlockSpec((1, ts, d), lambda bi, j: (bi, j, 0)),
            _const_spec((1, d)),
            _const_spec((d, cols)),
            _const_spec((rank_pad, qk)),
            _const_spec((1, qk)),
            _const_spec((1, vdim // GLA_HEADS)),
            _const_spec((CONV_W, lru)),
            _const_spec((1, lru)),
            _const_spec((lru, 2 * lru)),
            _const_spec((1, 2 * lru)),
            _const_spec((1, lru)),
            _const_spec((vdim + lru, d)),
        ],
        out_specs=pl.BlockSpec((1, ts, d), lambda bi, j: (bi, j, 0)),
        out_shape=jax.ShapeDtypeStruct((b, s, d), F32),
        scratch_shapes=[
            pltpu.VMEM((GLA_HEADS // GLA_HEADS_PER_GROUP, GLA_HEADS_PER_GROUP * (vdim // GLA_HEADS),
                        GLA_HEADS_PER_GROUP * (qk // GLA_HEADS)), F32),
            pltpu.VMEM((SUBLANE + ts, lru), F32),
            pltpu.VMEM((1, lru), F32),
        ],
        compiler_params=pltpu.CompilerParams(
            dimension_semantics=("parallel", "arbitrary"), vmem_limit_bytes=VMEM_LIMIT),
        name="mixer_gla_rglru",
    )(x3, nrm, win, gup, gbias, hnorm, convw, convb, wgate, bgate, lam, wout)


HG_HEADS_PER_GROUP = 2


def _mix1_kernel(x_ref, nrm_ref, win_ref, lbl_ref, hnorm_ref, wout_ref, o_ref, s_ref, *, layer, hk, hv):
    j = pl.program_id(1)

    @pl.when(j == 0)
    def _():
        s_ref[...] = jnp.zeros_like(s_ref)

    ts = x_ref.shape[1]
    tsub = ts // SUBTILES
    x = x_ref[0]
    h = _rms(x, nrm_ref[...]).astype(BF16)
    zs = [_dot(h[r0:r0 + tsub], win_ref[...]) for r0 in range(0, ts, tsub)]

    logits = lbl_ref[...]
    m = jnp.max(logits, axis=0, keepdims=True)
    e = jnp.exp(logits - m)
    sm = e / jnp.sum(e, axis=0, keepdims=True)
    cs = sm[0:1, :]
    for li in range(1, layer + 1):
        cs = cs + sm[li:li + 1, :]
    lb = cs - sm[0:1, :]

    heads = hk // HG_DK
    for si, r0 in enumerate(range(0, ts, tsub)):
        z = zs[si]
        q = z[:, 0:hk]
        fz = z[:, hk:2 * hk]
        iv = z[:, 2 * hk:2 * hk + hv]
        g = z[:, 2 * hk + hv:2 * hk + 2 * hv]
        sg = _sigmoid(fz)
        f = lb + (1.0 - lb) * sg
        k = (1.0 - lb) * (1.0 - sg)
        dec = _chunk_suffix_prod(f)
        o = _chunk_state_attention(
            (q * _sigmoid(q)).astype(BF16), (k * dec).astype(BF16), iv.astype(BF16), f * dec, s_ref,
            heads=heads, dk=HG_DK, dv=HG_DV, heads_per_group=HG_HEADS_PER_GROUP)
        o = _head_rmsnorm(o, hnorm_ref[...], heads, HG_DV) * (g * _sigmoid(g))
        o_ref[0, r0:r0 + tsub, :] = x[r0:r0 + tsub] + _dot(o.astype(BF16), wout_ref[...])


def _mixer1(x3, nrm, win, lbl, hnorm, wout, *, layer):
    b, s, d = x3.shape
    ts = SEQ_TILE
    cols = win.shape[1]
    hv = wout.shape[0]
    hk = (cols - 2 * hv) // 2
    heads = hk // HG_DK
    groups = heads // HG_HEADS_PER_GROUP
    kern = functools.partial(_mix1_kernel, layer=layer, hk=hk, hv=hv)
    return pl.pallas_call(
        kern,
        grid=(b, s // ts),
        in_specs=[
            pl.BlockSpec((1, ts, d), lambda bi, j: (bi, j, 0)),
            _const_spec((1, d)),
            _const_spec((d, cols)),
            _const_spec(lbl.shape),
            _const_spec((1, HG_DV)),
            _const_spec((hv, d)),
        ],
        out_specs=pl.BlockSpec((1, ts, d), lambda bi, j: (bi, j, 0)),
        out_shape=jax.ShapeDtypeStruct((b, s, d), F32),
        scratch_shapes=[
            pltpu.VMEM((groups, HG_HEADS_PER_GROUP * HG_DV, HG_HEADS_PER_GROUP * HG_DK), F32),
        ],
        compiler_params=pltpu.CompilerParams(
            dimension_semantics=("parallel", "arbitrary"), vmem_limit_bytes=VMEM_LIMIT),
        name="mixer_hgrn2",
    )(x3, nrm, win, lbl, hnorm, wout)


def _row(v):
    return v.reshape(1, -1).astype(F32)


def _block_diag_dense(w):
    g, n, _ = w.shape
    eye = jnp.eye(g, dtype=w.dtype)
    return (eye[:, None, :, None] * w[:, :, None, :]).reshape(g * n, g * n)


def kernel(x, p, ffn_norm, ffn_w1, ffn_w3, ffn_w2, mix_norm, ple_norm, ple_proj, ple_gate, final_norm,
           m0_w_in, gla_gate_up, gla_gate_bias, gla_head_norm, lru_conv_w, lru_conv_b, lru_wa, lru_ba,
           lru_wx, lru_bx, lru_lambda, m0_w_out, m1_w_in, hgrn_lb_logits, hgrn_head_norm, m1_w_out):
    bsz, seq, d = x.shape
    depth = p.shape[0]
    t = bsz * seq
    x2 = x.reshape(t, d)

    qk = gla_gate_up.shape[2]
    rank = gla_gate_up.shape[1]
    lru = lru_lambda.shape[1]
    vdim = m0_w_out.shape[1] - lru
    rank_pad = LANE

    ff = ffn_w1.shape[-1]
    fnorms = ffn_norm.reshape(depth * 2, 1, d)
    fw1 = ffn_w1.astype(BF16).reshape(depth * 2, d, ff)
    fw3 = ffn_w3.astype(BF16).reshape(depth * 2, d, ff)
    fw2 = ffn_w2.astype(BF16).reshape(depth * 2, ff, d)
    p3 = p.reshape(depth, t, p.shape[-1])
    pnorms = ple_norm.reshape(depth, 1, d)
    pgate = ple_gate.astype(BF16)
    pproj = ple_proj.astype(BF16)

    for layer in range(depth):
        x2 = _ffn(x2, fnorms, fw1, fw3, fw2, 2 * layer)
        if layer % 2 == 0:
            e = layer // 2
            w = m0_w_in[e]
            split = 2 * qk + 2 * vdim
            win = jnp.concatenate([
                w[:, :split],
                jnp.pad(w[:, split:split + rank], ((0, 0), (0, rank_pad - rank))),
                w[:, split + rank:],
            ], axis=1).astype(BF16)
            gup = jnp.pad(gla_gate_up[e], ((0, rank_pad - rank), (0, 0))).astype(BF16)
            wgate = jnp.concatenate([_block_diag_dense(lru_wa[e]), _block_diag_dense(lru_wx[e])],
                                    axis=1).astype(BF16)
            bgate = jnp.concatenate([lru_ba[e], lru_bx[e]]).reshape(1, -1)
            x3 = _mixer0(x2.reshape(bsz, seq, d), _row(mix_norm[layer]), win, gup, _row(gla_gate_bias[e]),
                         _row(gla_head_norm[e]), lru_conv_w[e], _row(lru_conv_b[e]), wgate, bgate,
                         _row(lru_lambda[e]), m0_w_out[e].astype(BF16),
                         qk=qk, vdim=vdim, lru=lru, rank_pad=rank_pad)
        else:
            o = layer // 2
            x3 = _mixer1(x2.reshape(bsz, seq, d), _row(mix_norm[layer]), m1_w_in[o].astype(BF16),
                         hgrn_lb_logits, _row(hgrn_head_norm[o]), m1_w_out[o].astype(BF16), layer=layer)
        x2 = x3.reshape(t, d)
        x2 = _ffn(x2, fnorms, fw1, fw3, fw2, 2 * layer + 1,
                  ple_args=(p3, pnorms, pgate, pproj, _row(final_norm), layer, layer == depth - 1))
    return x2.reshape(bsz, seq, d)
```
